```python
import math
import jax
import jax.numpy as jnp
from jax import lax
import numpy as np

D_MODEL = 1024
BATCH = 8
SEQ = 2048
DEPTH = 4
DEC_BATCH = 128
DEC_SEQ = 8
PAST_LEN = 2048
PAGE_SIZE = 128

RWKV_HEADS = 12
RWKV_HS = 64
W_RWKV = RWKV_HEADS * RWKV_HS
LORA_W = 64
LORA_A = 64
LORA_V = 32
LNX_EPS = 64e-5
SHIFT_FIRST = 3 * W_RWKV + LORA_W + LORA_A
SHIFT_REST = SHIFT_FIRST + LORA_V
DIFF_HEADS = 6
DIFF_HD = 64
W_DIFF = DIFF_HEADS * 2 * DIFF_HD
MEM_TOKENS = 256
MEM_HEADS = 4
MEM_HD = 64
W_MEM = MEM_HEADS * MEM_HD
MIX = W_RWKV + W_MEM
ROPE_THETA = 10000.0
Q_BLOCK = 128
NORM_EPS = 1e-6
NEG_INF = -1e30

kernel_name = "yoco_rwkv7_diffattn_mem_decoder_step"


def rms_norm(x, g, eps=NORM_EPS):
    xf = x.astype(jnp.float32)
    y = xf * lax.rsqrt(jnp.mean(xf * xf, axis=-1, keepdims=True) + eps)
    return (y * g.astype(jnp.float32)).astype(x.dtype)


def rope(x, pos):
    half = x.shape[-1] // 2
    inv = ROPE_THETA ** (-jnp.arange(half, dtype=jnp.float32) / half)
    ang = pos.astype(jnp.float32)[:, None] * inv[None, :]
    shape = (1, pos.shape[0]) + (1,) * (x.ndim - 3) + (half,)
    cos, sin = jnp.cos(ang).reshape(shape), jnp.sin(ang).reshape(shape)
    xf = x.astype(jnp.float32)
    x1, x2 = xf[..., :half], xf[..., half:]
    return jnp.concatenate([x1 * cos - x2 * sin, x2 * cos + x1 * sin], axis=-1).astype(x.dtype)


def wkv_scan(s0, r, decay, k, v, kk, a):
    def step(s, inp):
        r_t, w_t, k_t, v_t, kk_t, a_t = inp
        sa = jnp.einsum("bhij,bhj->bhi", s, -kk_t)
        s = (s * w_t[:, :, None, :] + sa[..., None] * (kk_t * a_t)[:, :, None, :]
             + v_t[..., None] * k_t[:, :, None, :])
        return s, jnp.einsum("bhij,bhj->bhi", s, r_t)
    xs = tuple(jnp.moveaxis(z, 1, 0) for z in (r, decay, k, v, kk, a))
    s, o = lax.scan(step, s0, xs)
    return s, jnp.moveaxis(o, 0, 1)


def mem_kv(mem, g, w_kv, k_g):
    bsz, m, _ = mem.shape
    k, v = jnp.split(rms_norm(mem, g) @ w_kv, [W_MEM], axis=-1)
    return (rms_norm(k.reshape(bsz, m, MEM_HEADS, MEM_HD), k_g),
            v.reshape(bsz, m, MEM_HEADS, MEM_HD))


def mem_attend(q, mem_k, mem_v, q_g):
    bsz, t, _ = q.shape
    qh = rms_norm(q.reshape(bsz, t, MEM_HEADS, MEM_HD), q_g)
    s = jnp.einsum("btnd,bmnd->bntm", qh, mem_k).astype(jnp.float32) * MEM_HD ** -0.5
    p = jax.nn.softmax(s, axis=-1)
    o = jnp.einsum("bntm,bmnd->btnd", p.astype(mem_v.dtype), mem_v)
    return o.reshape(bsz, t, W_MEM)


def diff_scores(q, k):
    return jnp.einsum("bqhcd,bkhcd->bhcqk", q, k).astype(jnp.float32) * DIFF_HD ** -0.5


def diff_weights(s, lam):
    p = jax.nn.softmax(s, axis=-1)
    return p[:, :, 0] - lam * p[:, :, 1]


def diff_attn_prompt(q, k, v, lam):
    bsz, t = q.shape[0], q.shape[1]
    kpos = jnp.arange(t)
    def block(i):
        q_b = lax.dynamic_slice_in_dim(q, i * Q_BLOCK, Q_BLOCK, axis=1)
        qpos = i * Q_BLOCK + jnp.arange(Q_BLOCK)
        s = jnp.where(qpos[:, None] >= kpos[None, :], diff_scores(q_b, k), NEG_INF)
        w = diff_weights(s, lam)
        return jnp.einsum("bhqk,bkhe->bqhe", w.astype(v.dtype), v)
    o = lax.map(block, jnp.arange(t // Q_BLOCK))
    return jnp.moveaxis(o, 0, 1).reshape(bsz, t, DIFF_HEADS, 2 * DIFF_HD)


def diff_attn_sample(q, k, v, lam, past_k, past_v):
    t = q.shape[1]
    n_past = past_k.shape[1]
    causal = jnp.arange(t)[:, None] >= jnp.arange(t)[None, :]
    s = jnp.concatenate([diff_scores(q, past_k),
                         jnp.where(causal, diff_scores(q, k), NEG_INF)], axis=-1)
    w = diff_weights(s, lam)
    return (jnp.einsum("bhqk,bkhe->bqhe", w[..., :n_past].astype(v.dtype), past_v)
            + jnp.einsum("bhqk,bkhe->bqhe", w[..., n_past:].astype(v.dtype), v))


def gather_pages(pool, page_table):
    rows = jnp.take(pool, page_table, axis=0)
    return rows.reshape((page_table.shape[0], page_table.shape[1] * pool.shape[1]) + pool.shape[2:])


def a_layer(h, prev_xn, s0, v_first, mem_k, mem_v, norm_g, w_in, mu, w0, w_up, a0, a_up,
            k_k, k_a, r_k, lnx_g, lnx_b, mq_g, w_out, v0=None, v_up=None):
    bsz, t, _ = h.shape
    f32 = jnp.float32
    xn = rms_norm(h, norm_g)
    proj = jnp.concatenate([prev_xn[:, None, :].astype(xn.dtype), xn], axis=1) @ w_in
    n_sh = mu.shape[0]
    cur = proj[:, 1:, :n_sh]
    mixed = cur + (proj[:, :-1, :n_sh] - cur) * mu
    mem_q, gate = jnp.split(proj[:, 1:, n_sh:], [W_MEM], axis=-1)
    r, k, v, wd, ad, vd = jnp.split(
        mixed, [W_RWKV, 2 * W_RWKV, 3 * W_RWKV, 3 * W_RWKV + LORA_W, SHIFT_FIRST], axis=-1)
    log_w = -jax.nn.softplus(-(w0 + jnp.tanh(wd) @ w_up).astype(f32)) - 0.5
    decay = jnp.exp(-jnp.exp(log_w))
    a = jax.nn.sigmoid((a0 + ad @ a_up).astype(f32))
    if v0 is None:
        v_first = v
    else:
        v = v + (v_first - v) * jax.nn.sigmoid(v0 + vd @ v_up)
    hd = lambda z: z.reshape(bsz, t, RWKV_HEADS, RWKV_HS).astype(f32)
    hp = lambda z: z.reshape(RWKV_HEADS, RWKV_HS).astype(f32)
    r, k, v, decay, a = hd(r), hd(k), hd(v), hd(decay), hd(a)
    kk = k * hp(k_k)
    kk = kk / jnp.maximum(jnp.sqrt(jnp.sum(kk * kk, axis=-1, keepdims=True)), 1e-12)
    k = k * (1.0 + (a - 1.0) * hp(k_a))
    s, o = wkv_scan(s0.astype(f32), r, decay, k, v, kk, a)
    mu_o = jnp.mean(o, axis=-1, keepdims=True)
    var_o = jnp.mean(jnp.square(o - mu_o), axis=-1, keepdims=True)
    o = (o - mu_o) * lax.rsqrt(var_o + LNX_EPS) * hp(lnx_g) + hp(lnx_b)
    o = o + jnp.sum(r * k * hp(r_k), axis=-1, keepdims=True) * v
    o = o.reshape(bsz, t, W_RWKV).astype(h.dtype)
    mo = mem_attend(mem_q, mem_k, mem_v, mq_g).astype(h.dtype)
    y = jnp.concatenate([o, mo], axis=-1) * jax.nn.silu(gate)
    return h + (y @ w_out).astype(h.dtype), xn[:, -1], s.astype(s0.dtype), v_first


def shared_kv(h, pos, g, w_kv, k_g):
    bsz, t, _ = h.shape
    k, v = jnp.split(rms_norm(h, g) @ w_kv, [W_DIFF], axis=-1)
    k = rope(rms_norm(k.reshape(bsz, t, DIFF_HEADS, 2, DIFF_HD), k_g), pos)
    return k, v.reshape(bsz, t, DIFF_HEADS, 2 * DIFF_HD)


def b_layer(h, pos, attend, mem_k, mem_v, norm_g, w_in, q_g, lam_v, lam_init, subln_g, mq_g, w_out):
    bsz, t, _ = h.shape
    xn = rms_norm(h, norm_g)
    q, mem_q, gate = jnp.split(xn @ w_in, [W_DIFF, W_DIFF + W_MEM], axis=-1)
    q = rope(rms_norm(q.reshape(bsz, t, DIFF_HEADS, 2, DIFF_HD), q_g), pos)
    lv = lam_v.astype(jnp.float32)
    lam = jnp.exp(jnp.sum(lv[0] * lv[1])) - jnp.exp(jnp.sum(lv[2] * lv[3])) + lam_init
    o = attend(q, lam)
    o = (rms_norm(o, subln_g) * (1.0 - lam_init)).reshape(bsz, t, W_DIFF).astype(h.dtype)
    mo = mem_attend(mem_q, mem_k, mem_v, mq_g).astype(h.dtype)
    y = jnp.concatenate([o, mo], axis=-1) * jax.nn.silu(gate)
    return h + (y @ w_out).astype(h.dtype)


def trunk(h, pos, mem_layers, shift0, wkv0, attend_fn, p):
    n_a = DEPTH // 2
    shifts, states = [], []
    v_first = None
    k_new = v_new = attend = None
    for l in range(DEPTH):
        mk, mv = mem_layers[l]
        if l < n_a:
            if l == 0:
                w_in, mu, v0, v_up = p["a_w_in_first"], p["a_mu_first"], None, None
            else:
                w_in, mu = p["a_w_in_rest"][l - 1], p["a_mu_rest"][l - 1]
                v0, v_up = p["a_v0"][l - 1], p["a_v_up"][l - 1]
            h, sh, st, v_first = a_layer(
                h, shift0[l], wkv0[l], v_first, mk, mv, p["norm_g"][l], w_in, mu,
                p["a_w0"][l], p["a_w_up"][l], p["a_a0"][l], p["a_a_up"][l], p["a_k_k"][l],
                p["a_k_a"][l], p["a_r_k"][l], p["a_lnx_g"][l], p["a_lnx_b"][l],
                p["mem_q_g"][l], p["w_out"][l], v0, v_up)
            shifts.append(sh)
            states.append(st)
            if l == n_a - 1:
                k_new, v_new = shared_kv(h, pos, p["kv_norm_g"], p["kv_w"], p["kv_k_g"])
                attend = lambda q, lam: attend_fn(q, k_new, v_new, lam)
        else:
            j = l - n_a
            lam_init = 0.8 - 0.6 * math.exp(-0.3 * l)
            h = b_layer(h, pos, attend, mk, mv, p["norm_g"][l], p["b_w_in"][j], p["b_q_g"][j],
                        p["b_lam"][j], lam_init, p["b_subln_g"][j], p["mem_q_g"][l], p["w_out"][l])
    return h, jnp.stack(shifts), jnp.stack(states), k_new, v_new


def setup_inputs(seed: int = 0) -> dict:
    key = jax.random.key(seed)
    ks = iter(jax.random.split(key, 64))
    f32 = jnp.float32
    def nrm(shape, scale):
        return jax.random.normal(next(ks), shape, f32) * scale
    def unif(shape, lo, hi):
        return jax.random.uniform(next(ks), shape, f32, lo, hi)
    def gain(shape):
        return 1.0 + nrm(shape, 0.02)
    n_a = DEPTH // 2
    n_b = DEPTH - n_a
    n_pages = PAST_LEN // PAGE_SIZE
    n_used = DEC_BATCH * n_pages
    n_pool = n_used + max(1, n_used // 4)
    c_first = SHIFT_FIRST + W_MEM + MIX
    c_rest = SHIFT_REST + W_MEM + MIX
    d_inv = D_MODEL ** -0.5
    return {
        "x_prompt": nrm((BATCH, SEQ, D_MODEL), 1.0),
        "x_sample": nrm((DEC_BATCH, DEC_SEQ, D_MODEL), 1.0),
        "mem_prompt": nrm((BATCH, MEM_TOKENS, D_MODEL), 1.0),
        "state_wkv": nrm((n_a, DEC_BATCH, RWKV_HEADS, RWKV_HS, RWKV_HS), 0.3),
        "state_shift": nrm((n_a, DEC_BATCH, D_MODEL), 1.0),
        "cache_k": nrm((n_pool, PAGE_SIZE, DIFF_HEADS, 2, DIFF_HD), 1.0),
        "cache_v": nrm((n_pool, PAGE_SIZE, DIFF_HEADS, 2 * DIFF_HD), 1.0),
        "cache_mem_k": nrm((DEPTH, DEC_BATCH, MEM_TOKENS, MEM_HEADS, MEM_HD), 1.0),
        "cache_mem_v": nrm((DEPTH, DEC_BATCH, MEM_TOKENS, MEM_HEADS, MEM_HD), 1.0),
        "page_table": jax.random.permutation(next(ks), n_pool)[:n_used].reshape(DEC_BATCH, n_pages).astype(jnp.int32),
        "norm_g": gain((DEPTH, D_MODEL)),
        "a_w_in_first": nrm((D_MODEL, c_first), d_inv),
        "a_mu_first": unif((SHIFT_FIRST,), 0.1, 0.9),
        "a_w_in_rest": nrm((n_a - 1, D_MODEL, c_rest), d_inv),
        "a_mu_rest": unif((n_a - 1, SHIFT_REST), 0.1, 0.9),
        "a_w0": unif((n_a, W_RWKV), -4.0, 0.0),
        "a_w_up": nrm((n_a, LORA_W, W_RWKV), 0.1),
        "a_a0": nrm((n_a, W_RWKV), 0.5),
        "a_a_up": nrm((n_a, LORA_A, W_RWKV), 0.5 * LORA_A ** -0.5),
        "a_v0": nrm((n_a - 1, W_RWKV), 0.5),
        "a_v_up": nrm((n_a - 1, LORA_V, W_RWKV), 0.5 * LORA_V ** -0.5),
        "a_k_k": 1.0 + nrm((n_a, W_RWKV), 0.1),
        "a_k_a": 1.0 + nrm((n_a, W_RWKV), 0.1),
        "a_r_k": nrm((n_a, W_RWKV), 0.1),
        "a_lnx_g": gain((n_a, W_RWKV)),
        "a_lnx_b": nrm((n_a, W_RWKV), 0.01),
        "kv_norm_g": gain((D_MODEL,)),
        "kv_w": nrm((D_MODEL, 2 * W_DIFF), d_inv),
        "kv_k_g": gain((DIFF_HD,)),
        "b_w_in": nrm((n_b, D_MODEL, W_DIFF + W_MEM + MIX), d_inv),
        "b_q_g": gain((n_b, DIFF_HD)),
        "b_lam": nrm((n_b, 4, DIFF_HD), 0.1),
        "b_subln_g": gain((n_b, 2 * DIFF_HD)),
        "mem_norm_g": gain((DEPTH, D_MODEL)),
        "mem_w_kv": nrm((DEPTH, D_MODEL, 2 * W_MEM), d_inv),
        "mem_k_g": gain((DEPTH, MEM_HD)),
        "mem_q_g": gain((DEPTH, MEM_HD)),
        "w_out": nrm((DEPTH, MIX, D_MODEL), MIX ** -0.5),
    }


def reference(x_prompt, x_sample, mem_prompt, state_wkv, state_shift, cache_k, cache_v,
              cache_mem_k, cache_mem_v, page_table, norm_g, a_w_in_first, a_mu_first,
              a_w_in_rest, a_mu_rest, a_w0, a_w_up, a_a0, a_a_up, a_v0, a_v_up, a_k_k, a_k_a,
              a_r_k, a_lnx_g, a_lnx_b, kv_norm_g, kv_w, kv_k_g, b_w_in, b_q_g, b_lam, b_subln_g,
              mem_norm_g, mem_w_kv, mem_k_g, mem_q_g, w_out):
    p = dict(norm_g=norm_g, a_w_in_first=a_w_in_first, a_mu_first=a_mu_first,
             a_w_in_rest=a_w_in_rest, a_mu_rest=a_mu_rest, a_w0=a_w0, a_w_up=a_w_up, a_a0=a_a0,
             a_a_up=a_a_up, a_v0=a_v0, a_v_up=a_v_up, a_k_k=a_k_k, a_k_a=a_k_a, a_r_k=a_r_k,
             a_lnx_g=a_lnx_g, a_lnx_b=a_lnx_b, kv_norm_g=kv_norm_g, kv_w=kv_w, kv_k_g=kv_k_g,
             b_w_in=b_w_in, b_q_g=b_q_g, b_lam=b_lam, b_subln_g=b_subln_g, mem_q_g=mem_q_g,
             w_out=w_out)
    n_a = DEPTH // 2
    bsz, seq, _ = x_prompt.shape
    dec_seq = x_sample.shape[1]
    past_len = page_table.shape[1] * cache_k.shape[1]

    mem_p = [mem_kv(mem_prompt, mem_norm_g[l], mem_w_kv[l], mem_k_g[l]) for l in range(DEPTH)]
    shift0_p = jnp.zeros((n_a, bsz, D_MODEL), x_prompt.dtype)
    wkv0_p = jnp.zeros((n_a, bsz, RWKV_HEADS, RWKV_HS, RWKV_HS), x_prompt.dtype)
    y_prompt, shift_p, wkv_p, k_p, v_p = trunk(
        x_prompt, jnp.arange(seq), mem_p, shift0_p, wkv0_p, diff_attn_prompt, p)
    mem_k_p = jnp.stack([m[0] for m in mem_p])
    mem_v_p = jnp.stack([m[1] for m in mem_p])

    past_k = gather_pages(cache_k, page_table)
    past_v = gather_pages(cache_v, page_table)
    mem_s = [(cache_mem_k[l], cache_mem_v[l]) for l in range(DEPTH)]
    attend_s = lambda q, k, v, lam: diff_attn_sample(q, k, v, lam, past_k, past_v)
    y_sample, shift_s, wkv_s, k_s, v_s = trunk(
        x_sample, past_len + jnp.arange(dec_seq), mem_s, state_shift, state_wkv, attend_s, p)

    return (y_prompt, y_sample, wkv_p, shift_p, k_p, v_p, mem_k_p, mem_v_p,
            wkv_s, shift_s, k_s, v_s)
```

```python
import functools
import math

import jax
import jax.numpy as jnp
from jax import lax
from jax.experimental import pallas as pl
from jax.experimental.pallas import tpu as pltpu

F32 = jnp.float32
BF16 = jnp.bfloat16

D_MODEL = 1024
DEPTH = 4
RWKV_HEADS = 12
RWKV_HS = 64
W_RWKV = RWKV_HEADS * RWKV_HS
LORA_W = 64
LORA_A = 64
LORA_V = 32
LNX_EPS = 64e-5
SHIFT_FIRST = 3 * W_RWKV + LORA_W + LORA_A
SHIFT_REST = SHIFT_FIRST + LORA_V
SHIFT_REST_PAD = SHIFT_FIRST + 128
DIFF_HEADS = 6
DIFF_HD = 64
W_DIFF = DIFF_HEADS * 2 * DIFF_HD
MEM_HEADS = 4
MEM_HD = 64
W_MEM = MEM_HEADS * MEM_HD
MIX = W_RWKV + W_MEM
TAIL = W_MEM + MIX
ROPE_THETA = 10000.0
NORM_EPS = 1e-6
NEG_INF = -1e30
LANES = 128
VMEM_LIMIT_BYTES = 48 * 1024 * 1024

_NT = (((1,), (1,)), ((), ()))


def _params(*sem):
    return pltpu.CompilerParams(dimension_semantics=sem, vmem_limit_bytes=VMEM_LIMIT_BYTES)


def _seg_sum64(x):
    m, c = x.shape
    left = lax.broadcasted_iota(jnp.int32, (m, LANES), 1) < 64
    outs = []
    for p in range(c // LANES):
        xb = x[:, LANES * p:LANES * (p + 1)]
        sl = jnp.sum(jnp.where(left, xb, 0.0), axis=-1, keepdims=True)
        sr = jnp.sum(jnp.where(left, 0.0, xb), axis=-1, keepdims=True)
        outs.append(jnp.where(left, sl, sr))
    return outs[0] if len(outs) == 1 else jnp.concatenate(outs, axis=-1)


def _norm_matmul_kernel(x_ref, g_ref, w_ref, *o_refs, do_norm, splits):
    x = x_ref[...]
    if do_norm:
        x = x * lax.rsqrt(jnp.mean(x * x, axis=-1, keepdims=True) + NORM_EPS) * g_ref[...]
    xb = x.astype(BF16)
    off = 0
    for o_ref, c in zip(o_refs, splits):
        o_ref[...] = jnp.dot(xb, w_ref[:, off:off + c], preferred_element_type=F32)
        off += c


def norm_matmul(x, g, w, splits, do_norm=True, tm=256):
    m = x.shape[0]
    tm = min(tm, m)
    assert m % tm == 0 and sum(splits) == w.shape[1]
    c = w.shape[1]
    return pl.pallas_call(
        functools.partial(_norm_matmul_kernel, do_norm=do_norm, splits=tuple(splits)),
        grid=(m // tm,),
        in_specs=[pl.BlockSpec((tm, D_MODEL), lambda i: (i, 0)),
                  pl.BlockSpec((1, D_MODEL), lambda i: (0, 0)),
                  pl.BlockSpec((D_MODEL, c), lambda i: (0, 0))],
        out_specs=[pl.BlockSpec((tm, s), lambda i: (i, 0)) for s in splits],
        out_shape=[jax.ShapeDtypeStruct((m, s), F32) for s in splits],
        compiler_params=_params("parallel"),
        name="norm_matmul",
    )(x, g.reshape(1, D_MODEL), w)


def _rms_rows_kernel(x_ref, g_ref, o_ref):
    x = x_ref[...]
    o_ref[...] = x * lax.rsqrt(jnp.mean(x * x, axis=-1, keepdims=True) + NORM_EPS) * g_ref[...]


def rms_rows(x, g):
    return pl.pallas_call(
        _rms_rows_kernel,
        out_shape=jax.ShapeDtypeStruct(x.shape, F32),
        name="rms_rows",
    )(x, g.reshape(1, -1))


def _a_prep_kernel(*refs, first):
    if first:
        (p_ref, p8_ref, p0_ref, mu_ref, w0_ref, wup_ref, a0_ref, aup_ref, kk_ref, ka_ref, rk_ref,
         r_o, w_o, k_o, v_o, kkn_o, b_o, bv_o) = refs
    else:
        (p_ref, p8_ref, p0_ref, mu_ref, w0_ref, wup_ref, a0_ref, aup_ref, kk_ref, ka_ref, rk_ref,
         v0_ref, vup_ref, vf_ref, r_o, w_o, k_o, v_o, kkn_o, b_o, bv_o) = refs
    first_tile = pl.program_id(1) == 0
    tm = p_ref.shape[0]
    row0 = lax.broadcasted_iota(jnp.int32, (tm, 1), 0) == 0

    def mixed(lo, hi):
        cur = p_ref[:, lo:hi]
        prow = jnp.where(first_tile, p0_ref[:, lo:hi], p8_ref[7:8, lo:hi])
        prev = jnp.where(row0, prow, pltpu.roll(cur, 1, 0))
        return cur + (prev - cur) * mu_ref[:, lo:hi]

    r = mixed(0, W_RWKV)
    k = mixed(W_RWKV, 2 * W_RWKV)
    v = mixed(2 * W_RWKV, 3 * W_RWKV)
    lora_in = mixed(3 * W_RWKV, SHIFT_FIRST)
    z = -(w0_ref[...] + jnp.dot(jnp.tanh(lora_in).astype(BF16), wup_ref[...],
                                preferred_element_type=F32))
    softplus = jnp.maximum(z, 0.0) + jnp.log(1.0 + jnp.exp(-jnp.abs(z)))
    decay = jnp.exp(-jnp.exp(-softplus - 0.5))
    a = jax.nn.sigmoid(a0_ref[...] + jnp.dot(lora_in.astype(BF16), aup_ref[...],
                                             preferred_element_type=F32))
    if not first:
        vd = mixed(SHIFT_FIRST, SHIFT_REST_PAD)
        gate = jax.nn.sigmoid(v0_ref[...] + jnp.dot(vd.astype(BF16), vup_ref[...],
                                                    preferred_element_type=F32))
        v = v + (vf_ref[...] - v) * gate
    kk = k * kk_ref[...]
    kk = kk / jnp.maximum(jnp.sqrt(_seg_sum64(kk * kk)), 1e-12)
    k = k * (1.0 + (a - 1.0) * ka_ref[...])
    r_o[...] = r
    w_o[...] = decay
    k_o[...] = k
    v_o[...] = v
    kkn_o[...] = kk
    b_o[...] = kk * a
    bv_o[...] = _seg_sum64(r * k * rk_ref[...]) * v


def a_prep(psh, p0, mu, w0, wup, a0, aup, k_k, k_a, r_k, v0=None, vup=None, v_first=None, tm=256):
    bsz, t, csh = psh.shape
    tm = min(tm, t)
    assert t % tm == 0 and tm % 8 == 0
    first = v0 is None
    row = lambda z: z.reshape(1, -1)
    tok = pl.BlockSpec((None, tm, W_RWKV), lambda b, i: (b, i, 0))
    const = lambda shape: pl.BlockSpec(shape, lambda b, i: (0, 0))
    in_specs = [pl.BlockSpec((None, tm, csh), lambda b, i: (b, i, 0)),
                pl.BlockSpec((None, 8, csh), lambda b, i: (b, jnp.maximum(i * (tm // 8) - 1, 0), 0)),
                pl.BlockSpec((None, 1, csh), lambda b, i: (b, 0, 0)),
                const((1, csh)), const((1, W_RWKV)), const((LANES, W_RWKV)), const((1, W_RWKV)),
                const((LANES, W_RWKV)), const((1, W_RWKV)), const((1, W_RWKV)), const((1, W_RWKV))]
    args = [psh, psh, p0, row(mu), row(w0), wup, row(a0), aup, row(k_k), row(k_a), row(r_k)]
    if not first:
        in_specs += [const((1, W_RWKV)), const((LANES, W_RWKV)), tok]
        args += [row(v0), vup, v_first]
    return pl.pallas_call(
        functools.partial(_a_prep_kernel, first=first),
        grid=(bsz, t // tm),
        in_specs=in_specs,
        out_specs=[tok] * 7,
        out_shape=[jax.ShapeDtypeStruct((bsz, t, W_RWKV), F32)] * 7,
        compiler_params=_params("parallel", "parallel"),
        name="a_prep",
    )(*args)


def _wkv_kernel(r_ref, w_ref, k_ref, v_ref, kk_ref, b_ref, s0_ref, o_ref, s_ref, *, bb, tc):
    @pl.when(pl.program_id(1) == 0)
    def _():
        s_ref[...] = s0_ref[...]

    eye = (lax.broadcasted_iota(jnp.int32, (RWKV_HS, RWKV_HS), 0)
           == lax.broadcasted_iota(jnp.int32, (RWKV_HS, RWKV_HS), 1)).astype(F32)

    row_id = lax.broadcasted_iota(jnp.int32, (8, RWKV_HS), 0)

    def block(t8, carry):
        base = pl.multiple_of(t8 * 8, 8)
        for bi in range(bb):
            tiles = [ref[bi, pl.ds(base, 8), :] for ref in (r_ref, w_ref, k_ref, v_ref, kk_ref, b_ref)]
            heads = [[z[:, h * RWKV_HS:(h + 1) * RWKV_HS] for z in tiles] for h in range(RWKV_HEADS)]
            outs = [jnp.zeros((8, RWKV_HS), F32)] * RWKV_HEADS
            for u in range(8):
                for h in range(RWKV_HEADS):
                    r_t, w_t, k_t, v_t, kk_t, b_t = [z[u:u + 1, :] for z in heads[h]]
                    s = s_ref[bi, h]
                    sa = jnp.sum(s * (-kk_t), axis=-1, keepdims=True)
                    v_col = jnp.sum(eye * v_t, axis=-1, keepdims=True)
                    s = s * w_t + sa * b_t + v_col * k_t
                    s_ref[bi, h] = s
                    o_col = jnp.sum(s * r_t, axis=-1, keepdims=True)
                    o_row = jnp.sum(eye * o_col, axis=0, keepdims=True)
                    outs[h] = jnp.where(row_id == u, o_row, outs[h])
            o_ref[bi, pl.ds(base, 8), :] = jnp.concatenate(outs, axis=-1)
        return carry

    lax.fori_loop(0, tc // 8, block, 0)


def wkv_scan(r, w, k, v, kk, b, s0, bb, tc):
    bsz, t, _ = r.shape
    assert bsz % bb == 0 and t % tc == 0
    tok = pl.BlockSpec((bb, tc, W_RWKV), lambda i, j: (i, j, 0))
    st = pl.BlockSpec((bb, RWKV_HEADS, RWKV_HS, RWKV_HS), lambda i, j: (i, 0, 0, 0))
    return pl.pallas_call(
        functools.partial(_wkv_kernel, bb=bb, tc=tc),
        grid=(bsz // bb, t // tc),
        in_specs=[tok] * 6 + [st],
        out_specs=[tok, st],
        out_shape=[jax.ShapeDtypeStruct((bsz, t, W_RWKV), F32),
                   jax.ShapeDtypeStruct(s0.shape, F32)],
        compiler_params=_params("parallel", "arbitrary"),
        name="wkv_scan",
    )(r, w, k, v, kk, b, s0)


def _mix_out_kernel(*refs, mode, bb, out_scale):
    if mode == "a":
        (o_ref, bv_ref, tail_ref, h_ref, mk_ref, mv_ref, wout_ref, mqg_ref, g_ref, bias_ref,
         out_ref) = refs
    else:
        o_ref, tail_ref, h_ref, mk_ref, mv_ref, wout_ref, mqg_ref, g_ref, out_ref = refs
    tq = o_ref.shape[1]
    rows = bb * tq
    o = o_ref[...].reshape(rows, W_RWKV)
    if mode == "a":
        d = o - _seg_sum64(o) * (1.0 / RWKV_HS)
        var = _seg_sum64(d * d) * (1.0 / RWKV_HS)
        o = d * lax.rsqrt(var + LNX_EPS) * g_ref[...] + bias_ref[...] + bv_ref[...].reshape(rows, W_RWKV)
    else:
        parts = []
        for hh in range(DIFF_HEADS):
            ob = o[:, LANES * hh:LANES * (hh + 1)]
            parts.append(ob * lax.rsqrt(jnp.mean(ob * ob, axis=-1, keepdims=True) + NORM_EPS))
        o = jnp.concatenate(parts, axis=-1) * g_ref[...] * out_scale

    tail = tail_ref[...].reshape(rows, TAIL)
    mq = tail[:, :W_MEM]
    gate = tail[:, W_MEM:]
    qn = mq * lax.rsqrt(_seg_sum64(mq * mq) * (1.0 / MEM_HD) + NORM_EPS) * mqg_ref[...]
    lane = lax.broadcasted_iota(jnp.int32, (tq, W_MEM), 1)
    mos = []
    for bi in range(bb):
        kb = mk_ref[bi].astype(BF16)
        vb = mv_ref[bi].astype(BF16)
        qb = qn[bi * tq:(bi + 1) * tq]
        mo = jnp.zeros((tq, W_MEM), F32)
        for n in range(MEM_HEADS):
            in_head = (lane >= n * MEM_HD) & (lane < (n + 1) * MEM_HD)
            s = lax.dot_general(jnp.where(in_head, qb, 0.0).astype(BF16), kb, _NT,
                                preferred_element_type=F32) * MEM_HD ** -0.5
            p = jnp.exp(s - jnp.max(s, axis=-1, keepdims=True))
            p = p / jnp.sum(p, axis=-1, keepdims=True)
            mo = jnp.where(in_head, jnp.dot(p.astype(BF16), vb, preferred_element_type=F32), mo)
        mos.append(mo)
    mo = mos[0] if bb == 1 else jnp.concatenate(mos, axis=0)

    silu = gate * jax.nn.sigmoid(gate)
    y_o = (o * silu[:, :W_RWKV]).astype(BF16)
    y_m = (mo * silu[:, W_RWKV:]).astype(BF16)
    res = (h_ref[...].reshape(rows, D_MODEL)
           + jnp.dot(y_o, wout_ref[0:W_RWKV, :], preferred_element_type=F32)
           + jnp.dot(y_m, wout_ref[W_RWKV:MIX, :], preferred_element_type=F32))
    out_ref[...] = res.reshape(bb, tq, D_MODEL)


def mix_out(mode, o, bv, tail, h, mem_k, mem_v, w_out, mq_g, g, bias, out_scale, bb, tq):
    bsz, t, _ = o.shape
    assert bsz % bb == 0 and t % tq == 0
    tok = lambda c: pl.BlockSpec((bb, tq, c), lambda i, j: (i, j, 0))
    const = lambda shape: pl.BlockSpec(shape, lambda i, j: (0, 0))
    mem = pl.BlockSpec((bb, mem_k.shape[1], W_MEM), lambda i, j: (i, 0, 0))
    row = lambda z: z.reshape(1, -1)
    in_specs = [tok(W_RWKV)] + ([tok(W_RWKV)] if mode == "a" else []) + [
        tok(TAIL), tok(D_MODEL), mem, mem, const((MIX, D_MODEL)), const((1, W_MEM)), const((1, W_RWKV))]
    args = [o] + ([bv] if mode == "a" else []) + [
        tail, h, mem_k, mem_v, w_out, row(jnp.tile(mq_g, MEM_HEADS)), row(g)]
    if mode == "a":
        in_specs.append(const((1, W_RWKV)))
        args.append(row(bias))
    return pl.pallas_call(
        functools.partial(_mix_out_kernel, mode=mode, bb=bb, out_scale=out_scale),
        grid=(bsz // bb, t // tq),
        in_specs=in_specs,
        out_specs=tok(D_MODEL),
        out_shape=jax.ShapeDtypeStruct((bsz, t, D_MODEL), F32),
        compiler_params=_params("parallel", "parallel"),
        name="mix_out_" + mode,
    )(*args)


def _qk_rope_kernel(*refs, rope):
    if rope:
        x_ref, g_ref, cos_ref, sin_ref, o_ref = refs
    else:
        x_ref, g_ref, o_ref = refs
    x = x_ref[...]
    y = x * lax.rsqrt(_seg_sum64(x * x) * (1.0 / 64) + NORM_EPS) * g_ref[...]
    if rope:
        tm, c = y.shape
        first_half = lax.broadcasted_iota(jnp.int32, (tm, LANES), 1) % 64 < 32
        parts = []
        for p in range(c // LANES):
            yb = y[:, LANES * p:LANES * (p + 1)]
            parts.append(jnp.where(first_half, pltpu.roll(yb, LANES - 32, 1), pltpu.roll(yb, 32, 1)))
        y = y * cos_ref[...] + jnp.concatenate(parts, axis=-1) * sin_ref[...]
    o_ref[...] = y


def qk_norm_rope(x, g, cos=None, sin=None, tm=256):
    bsz, t, c = x.shape
    tm = min(tm, t)
    assert t % tm == 0
    rope = cos is not None
    tok = pl.BlockSpec((None, tm, c), lambda b, i: (b, i, 0))
    in_specs = [tok, pl.BlockSpec((1, c), lambda b, i: (0, 0))]
    args = [x, jnp.tile(g, c // g.shape[0]).reshape(1, c)]
    if rope:
        in_specs += [pl.BlockSpec((tm, c), lambda b, i: (i, 0))] * 2
        args += [cos, sin]
    return pl.pallas_call(
        functools.partial(_qk_rope_kernel, rope=rope),
        grid=(bsz, t // tm),
        in_specs=in_specs,
        out_specs=tok,
        out_shape=jax.ShapeDtypeStruct(x.shape, F32),
        compiler_params=_params("parallel", "parallel"),
        name="qk_norm_rope",
    )(*args)


def rope_tables(pos, width):
    half = DIFF_HD // 2
    inv = ROPE_THETA ** (-jnp.arange(half, dtype=F32) / half)
    ang = pos.astype(F32)[:, None] * inv[None, :]
    cos, sin = jnp.cos(ang), jnp.sin(ang)
    reps = width // DIFF_HD
    return (jnp.tile(jnp.concatenate([cos, cos], axis=-1), (1, reps)),
            jnp.tile(jnp.concatenate([-sin, sin], axis=-1), (1, reps)))


def _online_update(s, m, l, acc, vb):
    m_new = jnp.maximum(m, jnp.max(s, axis=-1, keepdims=True))
    alpha = jnp.exp(m - m_new)
    p = jnp.exp(s - m_new)
    l = alpha * l + jnp.sum(p, axis=-1, keepdims=True)
    acc = alpha * acc + jnp.dot(p.astype(BF16), vb, preferred_element_type=F32)
    return m_new, l, acc


def _diff_attn_prompt_kernel(lam_ref, q_ref, k_ref, v_ref, o_ref, *, tq):
    i = pl.program_id(2)
    q = q_ref[...]
    left = lax.broadcasted_iota(jnp.int32, (tq, LANES), 1) < DIFF_HD
    qs = (jnp.where(left, q, 0.0).astype(BF16), jnp.where(left, 0.0, q).astype(BF16))
    qpos = i * tq + lax.broadcasted_iota(jnp.int32, (tq, tq), 0)
    kidx = lax.broadcasted_iota(jnp.int32, (tq, tq), 1)

    def body(j, carry):
        off = pl.multiple_of(j * tq, tq)
        kb = k_ref[pl.ds(off, tq), :].astype(BF16)
        vb = v_ref[pl.ds(off, tq), :].astype(BF16)
        causal = qpos >= j * tq + kidx
        out = []
        for c in range(2):
            s = lax.dot_general(qs[c], kb, _NT, preferred_element_type=F32) * DIFF_HD ** -0.5
            out.extend(_online_update(jnp.where(causal, s, NEG_INF), *carry[3 * c:3 * c + 3], vb))
        return tuple(out)

    init = (jnp.full((tq, 1), NEG_INF, F32), jnp.zeros((tq, 1), F32), jnp.zeros((tq, LANES), F32)) * 2
    _, l0, a0, _, l1, a1 = lax.fori_loop(0, i + 1, body, init)
    o_ref[...] = a0 / l0 - lam_ref[0] * (a1 / l1)


def diff_attn_prompt(q, k, v, lam, tq=256):
    bsz, t, _ = q.shape
    tq = min(tq, t)
    assert t % tq == 0
    qo = pl.BlockSpec((None, tq, LANES), lambda b, h, i: (b, i, h))
    kv = pl.BlockSpec((None, t, LANES), lambda b, h, i: (b, 0, h))
    return pl.pallas_call(
        functools.partial(_diff_attn_prompt_kernel, tq=tq),
        grid=(bsz, DIFF_HEADS, t // tq),
        in_specs=[pl.BlockSpec(memory_space=pltpu.SMEM), qo, kv, kv],
        out_specs=qo,
        out_shape=jax.ShapeDtypeStruct(q.shape, F32),
        compiler_params=_params("parallel", "parallel", "parallel"),
        name="diff_attn_prompt",
    )(lam.reshape(1), q, k, v)


def _diff_attn_sample_kernel(pt_ref, lam_ref, q_ref, kn_ref, vn_ref, ck_ref, cv_ref, o_ref,
                             qs_ref, m_ref, l_ref, acc_ref, *, n_pages):
    p = pl.program_id(1)
    t = q_ref.shape[0]
    n_rows = 2 * DIFF_HEADS * t

    @pl.when(p == 0)
    def _():
        q = q_ref[...]
        lane = lax.broadcasted_iota(jnp.int32, q.shape, 1)
        blocks = []
        for c in range(2):
            for h in range(DIFF_HEADS):
                lo = h * LANES + c * DIFF_HD
                blocks.append(jnp.where((lane >= lo) & (lane < lo + DIFF_HD), q, 0.0))
        qs_ref[...] = jnp.concatenate(blocks, axis=0).astype(BF16)
        m_ref[...] = jnp.full(m_ref.shape, NEG_INF, F32)
        l_ref[...] = jnp.zeros(l_ref.shape, F32)
        acc_ref[...] = jnp.zeros(acc_ref.shape, F32)

    def accumulate(kb, vb, mask):
        s = lax.dot_general(qs_ref[...], kb, _NT, preferred_element_type=F32) * DIFF_HD ** -0.5
        if mask is not None:
            s = jnp.where(mask, s, NEG_INF)
        m_ref[...], l_ref[...], acc_ref[...] = _online_update(s, m_ref[...], l_ref[...], acc_ref[...], vb)

    accumulate(ck_ref[...].astype(BF16), cv_ref[...].astype(BF16), None)

    @pl.when(p == n_pages - 1)
    def _():
        page = ck_ref.shape[0]
        pad = jnp.zeros((page - t, W_DIFF), F32)
        kn = jnp.concatenate([kn_ref[...], pad], axis=0).astype(BF16)
        vn = jnp.concatenate([vn_ref[...], pad], axis=0).astype(BF16)
        qt = lax.broadcasted_iota(jnp.int32, (n_rows, page), 0) % t
        kt = lax.broadcasted_iota(jnp.int32, (n_rows, page), 1)
        accumulate(kn, vn, qt >= kt)
        res = acc_ref[...] / l_ref[...]
        for h in range(DIFF_HEADS):
            r0, r1 = h * t, (DIFF_HEADS + h) * t
            cols = slice(h * LANES, (h + 1) * LANES)
            o_ref[:, cols] = res[r0:r0 + t, cols] - lam_ref[0] * res[r1:r1 + t, cols]


def diff_attn_sample(q, k_new, v_new, cache_k, cache_v, page_table, lam):
    bsz, t, _ = q.shape
    n_pages = page_table.shape[1]
    page = cache_k.shape[1]
    n_rows = 2 * DIFF_HEADS * t
    tok = pl.BlockSpec((None, t, W_DIFF), lambda b, p, pt: (b, 0, 0))
    pg = pl.BlockSpec((None, page, W_DIFF), lambda b, p, pt: (pt[b * n_pages + p], 0, 0))
    return pl.pallas_call(
        functools.partial(_diff_attn_sample_kernel, n_pages=n_pages),
        grid_spec=pltpu.PrefetchScalarGridSpec(
            num_scalar_prefetch=1,
            grid=(bsz, n_pages),
            in_specs=[pl.BlockSpec(memory_space=pltpu.SMEM), tok, tok, tok, pg, pg],
            out_specs=tok,
            scratch_shapes=[pltpu.VMEM((n_rows, W_DIFF), BF16), pltpu.VMEM((n_rows, 1), F32),
                            pltpu.VMEM((n_rows, 1), F32), pltpu.VMEM((n_rows, W_DIFF), F32)]),
        out_shape=jax.ShapeDtypeStruct(q.shape, F32),
        compiler_params=_params("parallel", "arbitrary"),
        name="diff_attn_sample",
    )(page_table.reshape(-1), lam.reshape(1), q, k_new, v_new,
      cache_k.reshape(cache_k.shape[0], page, W_DIFF), cache_v.reshape(cache_v.shape[0], page, W_DIFF))


def _pad_rows(w, rows_before, total=LANES):
    return jnp.pad(w, ((rows_before, total - rows_before - w.shape[0]), (0, 0))).astype(BF16)


def _trunk(h, pos, mem_layers, shift0, wkv0, attend_fn, p, scan_tiles, mix_tiles):
    bsz, t, _ = h.shape
    n_a = DEPTH // 2
    shifts, states = [], []
    v_first = None
    k_new = v_new = None
    cos, sin = rope_tables(pos, W_DIFF)
    for l in range(DEPTH):
        mk, mv = mem_layers[l]
        mk = mk.reshape(bsz, -1, W_MEM)
        mv = mv.reshape(bsz, -1, W_MEM)
        h2 = h.reshape(bsz * t, D_MODEL)
        w_out = p["w_out"][l].astype(BF16)
        if l < n_a:
            if l == 0:
                w_in = p["a_w_in_first"].astype(BF16)
                mu, csh = p["a_mu_first"], SHIFT_FIRST
                v0 = vup = None
            else:
                w = p["a_w_in_rest"][l - 1]
                w_in = jnp.concatenate(
                    [w[:, :SHIFT_REST], jnp.zeros((D_MODEL, SHIFT_REST_PAD - SHIFT_REST), F32),
                     w[:, SHIFT_REST:]], axis=1).astype(BF16)
                mu = jnp.pad(p["a_mu_rest"][l - 1], (0, SHIFT_REST_PAD - SHIFT_REST))
                csh = SHIFT_REST_PAD
                v0, vup = p["a_v0"][l - 1], _pad_rows(p["a_v_up"][l - 1], 0)
            psh, tail = norm_matmul(h2, p["norm_g"][l], w_in, (csh, TAIL))
            if shift0 is None:
                p0 = jnp.zeros((bsz, 1, csh), F32)
            else:
                (p0,) = norm_matmul(shift0[l], p["norm_g"][l], w_in[:, :csh], (csh,), do_norm=False)
                p0 = p0.reshape(bsz, 1, csh)
            r, w, k, v, kk, b, bv = a_prep(
                psh.reshape(bsz, t, csh), p0, mu, p["a_w0"][l], _pad_rows(p["a_w_up"][l], 0),
                p["a_a0"][l], _pad_rows(p["a_a_up"][l], LORA_W), p["a_k_k"][l], p["a_k_a"][l],
                p["a_r_k"][l], v0, vup, v_first)
            if l == 0:
                v_first = v
            s0 = jnp.zeros((bsz, RWKV_HEADS, RWKV_HS, RWKV_HS), F32) if wkv0 is None else wkv0[l]
            o, s_new = wkv_scan(r, w, k, v, kk, b, s0, *scan_tiles)
            states.append(s_new)
            shifts.append(rms_rows(h[:, -1, :], p["norm_g"][l]))
            h = mix_out("a", o, bv, tail.reshape(bsz, t, TAIL), h, mk, mv, w_out, p["mem_q_g"][l],
                        p["a_lnx_g"][l], p["a_lnx_b"][l], 1.0, *mix_tiles)
            if l == n_a - 1:
                k_raw, v_new = norm_matmul(h.reshape(bsz * t, D_MODEL), p["kv_norm_g"],
                                           p["kv_w"].astype(BF16), (W_DIFF, W_DIFF))
                k_new = qk_norm_rope(k_raw.reshape(bsz, t, W_DIFF), p["kv_k_g"], cos, sin)
                v_new = v_new.reshape(bsz, t, W_DIFF)
        else:
            j = l - n_a
            lam_init = 0.8 - 0.6 * math.exp(-0.3 * l)
            lv = p["b_lam"][j].astype(F32)
            lam = jnp.exp(jnp.sum(lv[0] * lv[1])) - jnp.exp(jnp.sum(lv[2] * lv[3])) + lam_init
            q_raw, tail = norm_matmul(h2, p["norm_g"][l], p["b_w_in"][j].astype(BF16), (W_DIFF, TAIL))
            q = qk_norm_rope(q_raw.reshape(bsz, t, W_DIFF), p["b_q_g"][j], cos, sin)
            o = attend_fn(q, k_new, v_new, lam)
            h = mix_out("b", o, None, tail.reshape(bsz, t, TAIL), h, mk, mv, w_out, p["mem_q_g"][l],
                        jnp.tile(p["b_subln_g"][j], DIFF_HEADS), None, 1.0 - lam_init, *mix_tiles)
    return h, jnp.stack(shifts), jnp.stack(states), k_new, v_new


def kernel(x_prompt, x_sample, mem_prompt, state_wkv, state_shift, cache_k, cache_v, cache_mem_k,
           cache_mem_v, page_table, norm_g, a_w_in_first, a_mu_first, a_w_in_rest, a_mu_rest, a_w0,
           a_w_up, a_a0, a_a_up, a_v0, a_v_up, a_k_k, a_k_a, a_r_k, a_lnx_g, a_lnx_b, kv_norm_g,
           kv_w, kv_k_g, b_w_in, b_q_g, b_lam, b_subln_g, mem_norm_g, mem_w_kv, mem_k_g, mem_q_g,
           w_out):
    p = dict(norm_g=norm_g, a_w_in_first=a_w_in_first, a_mu_first=a_mu_first,
             a_w_in_rest=a_w_in_rest, a_mu_rest=a_mu_rest, a_w0=a_w0, a_w_up=a_w_up, a_a0=a_a0,
             a_a_up=a_a_up, a_v0=a_v0, a_v_up=a_v_up, a_k_k=a_k_k, a_k_a=a_k_a, a_r_k=a_r_k,
             a_lnx_g=a_lnx_g, a_lnx_b=a_lnx_b, kv_norm_g=kv_norm_g, kv_w=kv_w, kv_k_g=kv_k_g,
             b_w_in=b_w_in, b_q_g=b_q_g, b_lam=b_lam, b_subln_g=b_subln_g, mem_q_g=mem_q_g,
             w_out=w_out)
    bsz, seq, _ = x_prompt.shape
    dbsz, dec_seq, _ = x_sample.shape
    m_tok = mem_prompt.shape[1]
    past_len = page_table.shape[1] * cache_k.shape[1]

    mem_p = []
    for l in range(DEPTH):
        k_raw, v_m = norm_matmul(mem_prompt.reshape(bsz * m_tok, D_MODEL), mem_norm_g[l],
                                 mem_w_kv[l].astype(BF16), (W_MEM, W_MEM))
        k_m = qk_norm_rope(k_raw.reshape(bsz, m_tok, W_MEM), mem_k_g[l])
        mem_p.append((k_m.reshape(bsz, m_tok, MEM_HEADS, MEM_HD), v_m.reshape(bsz, m_tok, MEM_HEADS, MEM_HD)))
    tq = min(256, seq)
    y_p, shift_p, wkv_p, k_p, v_p = _trunk(
        x_prompt, jnp.arange(seq), mem_p, None, None, diff_attn_prompt, p,
        scan_tiles=(1, tq), mix_tiles=(1, tq))

    mem_s = [(cache_mem_k[l], cache_mem_v[l]) for l in range(DEPTH)]
    attend_s = lambda q, k, v, lam: diff_attn_sample(q, k, v, cache_k, cache_v, page_table, lam)
    bb = math.gcd(dbsz, 16)
    y_s, shift_s, wkv_s, k_s, v_s = _trunk(
        x_sample, past_len + jnp.arange(dec_seq), mem_s, state_shift, state_wkv, attend_s, p,
        scan_tiles=(math.gcd(dbsz, 2), dec_seq), mix_tiles=(bb, dec_seq))

    return (y_p, y_s, wkv_p, shift_p,
            k_p.reshape(bsz, seq, DIFF_HEADS, 2, DIFF_HD), v_p.reshape(bsz, seq, DIFF_HEADS, 2 * DIFF_HD),
            jnp.stack([m[0] for m in mem_p]), jnp.stack([m[1] for m in mem_p]),
            wkv_s, shift_s,
            k_s.reshape(dbsz, dec_seq, DIFF_HEADS, 2, DIFF_HD),
            v_s.reshape(dbsz, dec_seq, DIFF_HEADS, 2 * DIFF_HD))
```

```python
import functools
import math

import jax
import jax.numpy as jnp
from jax import lax
from jax.experimental import pallas as pl
from jax.experimental.pallas import tpu as pltpu

F32 = jnp.float32
BF16 = jnp.bfloat16

D_MODEL = 1024
DEPTH = 4
RWKV_HEADS = 12
RWKV_HS = 64
W_RWKV = RWKV_HEADS * RWKV_HS
LORA_W = 64
LORA_A = 64
LORA_V = 32
LNX_EPS = 64e-5
SHIFT_FIRST = 3 * W_RWKV + LORA_W + LORA_A
SHIFT_REST = SHIFT_FIRST + LORA_V
SHIFT_REST_PAD = SHIFT_FIRST + 128
DIFF_HEADS = 6
DIFF_HD = 64
W_DIFF = DIFF_HEADS * 2 * DIFF_HD
MEM_HEADS = 4
MEM_HD = 64
W_MEM = MEM_HEADS * MEM_HD
MIX = W_RWKV + W_MEM
TAIL = W_MEM + MIX
ROPE_THETA = 10000.0
NORM_EPS = 1e-6
NEG_INF = -1e30
LANES = 128
VMEM_LIMIT_BYTES = 48 * 1024 * 1024
WKV_CHUNK = 64

_NT = (((1,), (1,)), ((), ()))


def _params(*sem):
    return pltpu.CompilerParams(dimension_semantics=sem, vmem_limit_bytes=VMEM_LIMIT_BYTES)


def _seg_sum64(x):
    m, c = x.shape
    left = lax.broadcasted_iota(jnp.int32, (m, LANES), 1) < 64
    outs = []
    for p in range(c // LANES):
        xb = x[:, LANES * p:LANES * (p + 1)]
        sl = jnp.sum(jnp.where(left, xb, 0.0), axis=-1, keepdims=True)
        sr = jnp.sum(jnp.where(left, 0.0, xb), axis=-1, keepdims=True)
        outs.append(jnp.where(left, sl, sr))
    return outs[0] if len(outs) == 1 else jnp.concatenate(outs, axis=-1)


def _norm_matmul_kernel(x_ref, g_ref, w_ref, *o_refs, do_norm, splits):
    x = x_ref[...]
    if do_norm:
        x = x * lax.rsqrt(jnp.mean(x * x, axis=-1, keepdims=True) + NORM_EPS) * g_ref[...]
    xb = x.astype(BF16)
    off = 0
    for o_ref, c in zip(o_refs, splits):
        o_ref[...] = jnp.dot(xb, w_ref[:, off:off + c], preferred_element_type=F32)
        off += c


def norm_matmul(x, g, w, splits, do_norm=True, tm=256):
    m = x.shape[0]
    tm = min(tm, m)
    assert m % tm == 0 and sum(splits) == w.shape[1]
    c = w.shape[1]
    return pl.pallas_call(
        functools.partial(_norm_matmul_kernel, do_norm=do_norm, splits=tuple(splits)),
        grid=(m // tm,),
        in_specs=[pl.BlockSpec((tm, D_MODEL), lambda i: (i, 0)),
                  pl.BlockSpec((1, D_MODEL), lambda i: (0, 0)),
                  pl.BlockSpec((D_MODEL, c), lambda i: (0, 0))],
        out_specs=[pl.BlockSpec((tm, s), lambda i: (i, 0)) for s in splits],
        out_shape=[jax.ShapeDtypeStruct((m, s), F32) for s in splits],
        compiler_params=_params("parallel"),
        name="norm_matmul",
    )(x, g.reshape(1, D_MODEL), w)


def _rms_rows_kernel(x_ref, g_ref, o_ref):
    x = x_ref[...]
    o_ref[...] = x * lax.rsqrt(jnp.mean(x * x, axis=-1, keepdims=True) + NORM_EPS) * g_ref[...]


def rms_rows(x, g):
    return pl.pallas_call(
        _rms_rows_kernel,
        out_shape=jax.ShapeDtypeStruct(x.shape, F32),
        name="rms_rows",
    )(x, g.reshape(1, -1))


def _a_prep_kernel(*refs, first):
    if first:
        (p_ref, p8_ref, p0_ref, mu_ref, w0_ref, wup_ref, a0_ref, aup_ref, kk_ref, ka_ref, rk_ref,
         r_o, w_o, k_o, v_o, kkn_o, b_o, bv_o) = refs
    else:
        (p_ref, p8_ref, p0_ref, mu_ref, w0_ref, wup_ref, a0_ref, aup_ref, kk_ref, ka_ref, rk_ref,
         v0_ref, vup_ref, vf_ref, r_o, w_o, k_o, v_o, kkn_o, b_o, bv_o) = refs
    first_tile = pl.program_id(1) == 0
    tm = p_ref.shape[0]
    row0 = lax.broadcasted_iota(jnp.int32, (tm, 1), 0) == 0

    def mixed(lo, hi):
        cur = p_ref[:, lo:hi]
        prow = jnp.where(first_tile, p0_ref[:, lo:hi], p8_ref[7:8, lo:hi])
        prev = jnp.where(row0, prow, pltpu.roll(cur, 1, 0))
        return cur + (prev - cur) * mu_ref[:, lo:hi]

    r = mixed(0, W_RWKV)
    k = mixed(W_RWKV, 2 * W_RWKV)
    v = mixed(2 * W_RWKV, 3 * W_RWKV)
    lora_in = mixed(3 * W_RWKV, SHIFT_FIRST)
    z = -(w0_ref[...] + jnp.dot(jnp.tanh(lora_in).astype(BF16), wup_ref[...],
                                preferred_element_type=F32))
    softplus = jnp.maximum(z, 0.0) + jnp.log(1.0 + jnp.exp(-jnp.abs(z)))
    log_decay = -jnp.exp(-softplus - 0.5)
    a = jax.nn.sigmoid(a0_ref[...] + jnp.dot(lora_in.astype(BF16), aup_ref[...],
                                             preferred_element_type=F32))
    if not first:
        vd = mixed(SHIFT_FIRST, SHIFT_REST_PAD)
        gate = jax.nn.sigmoid(v0_ref[...] + jnp.dot(vd.astype(BF16), vup_ref[...],
                                                    preferred_element_type=F32))
        v = v + (vf_ref[...] - v) * gate
    kk = k * kk_ref[...]
    kk = kk / jnp.maximum(jnp.sqrt(_seg_sum64(kk * kk)), 1e-12)
    k = k * (1.0 + (a - 1.0) * ka_ref[...])
    r_o[...] = r
    w_o[...] = log_decay
    k_o[...] = k
    v_o[...] = v
    kkn_o[...] = kk
    b_o[...] = kk * a
    bv_o[...] = _seg_sum64(r * k * rk_ref[...]) * v


def a_prep(psh, p0, mu, w0, wup, a0, aup, k_k, k_a, r_k, v0=None, vup=None, v_first=None, tm=256):
    bsz, t, csh = psh.shape
    tm = min(tm, t)
    assert t % tm == 0 and tm % 8 == 0
    first = v0 is None
    row = lambda z: z.reshape(1, -1)
    tok = pl.BlockSpec((None, tm, W_RWKV), lambda b, i: (b, i, 0))
    const = lambda shape: pl.BlockSpec(shape, lambda b, i: (0, 0))
    in_specs = [pl.BlockSpec((None, tm, csh), lambda b, i: (b, i, 0)),
                pl.BlockSpec((None, 8, csh), lambda b, i: (b, jnp.maximum(i * (tm // 8) - 1, 0), 0)),
                pl.BlockSpec((None, 1, csh), lambda b, i: (b, 0, 0)),
                const((1, csh)), const((1, W_RWKV)), const((LANES, W_RWKV)), const((1, W_RWKV)),
                const((LANES, W_RWKV)), const((1, W_RWKV)), const((1, W_RWKV)), const((1, W_RWKV))]
    args = [psh, psh, p0, row(mu), row(w0), wup, row(a0), aup, row(k_k), row(k_a), row(r_k)]
    if not first:
        in_specs += [const((1, W_RWKV)), const((LANES, W_RWKV)), tok]
        args += [row(v0), vup, v_first]
    return pl.pallas_call(
        functools.partial(_a_prep_kernel, first=first),
        grid=(bsz, t // tm),
        in_specs=in_specs,
        out_specs=[tok] * 7,
        out_shape=[jax.ShapeDtypeStruct((bsz, t, W_RWKV), F32)] * 7,
        compiler_params=_params("parallel", "parallel"),
        name="a_prep",
    )(*args)


_TN = (((0,), (0,)), ((), ()))


def _mm(a, b, dims=None):
    a, b = a.astype(BF16), b.astype(BF16)
    if dims is None:
        return jnp.dot(a, b, preferred_element_type=F32)
    return lax.dot_general(a, b, dims, preferred_element_type=F32)


def _unit_lower_inverse(a_list, ti, si, eye):
    n = a_list[0].shape[0]
    same_block = lambda m: (ti // m) == (si // m)
    a8 = a_list if n == 8 else [jnp.where(same_block(8), a, 0.0) for a in a_list]
    a8b = [a.astype(BF16) for a in a8]
    a2 = [_mm(a, a) for a in a8b]
    a4 = [_mm(x, x) for x in a2]
    inv = [_mm(eye - a, eye + x) for a, x in zip(a8, a2)]
    inv = [_mm(x, eye + y) for x, y in zip(inv, a4)]
    m = 8
    while m < n:
        in_pair = same_block(2 * m) & jnp.logical_not(same_block(m))
        invb = [x.astype(BF16) for x in inv]
        cross = [_mm(jnp.where(in_pair, a, 0.0), xb) for a, xb in zip(a_list, invb)]
        inv = [x - _mm(xb, y) for x, xb, y in zip(inv, invb, cross)]
        m *= 2
    return inv


def _wkv_kernel(r_ref, lw_ref, k_ref, v_ref, kk_ref, b_ref, s0_ref, o_ref, s_ref, *, bb, lc):
    @pl.when(pl.program_id(1) == 0)
    def _():
        s_ref[...] = s0_ref[...]

    ti = lax.broadcasted_iota(jnp.int32, (lc, lc), 0)
    si = lax.broadcasted_iota(jnp.int32, (lc, lc), 1)
    strict, incl = ti > si, ti >= si
    tri = incl.astype(BF16)
    eye = (ti == si).astype(F32)
    kk_h, r_h, k_h, b_h, v_h, ke_h, be_h, el_h = [], [], [], [], [], [], [], []
    for bi in range(bb):
        lw = lw_ref[bi]
        hi = lw.astype(BF16)
        rest = lw - hi.astype(F32)
        mid = rest.astype(BF16)
        low = (rest - mid.astype(F32)).astype(BF16)
        c = (jnp.dot(tri, hi, preferred_element_type=F32) + jnp.dot(tri, mid, preferred_element_type=F32)
             + jnp.dot(tri, low, preferred_element_type=F32))
        e_pos, e_neg = jnp.exp(c), jnp.exp(-c)
        k_i = k_ref[bi] * e_neg
        b_i = b_ref[bi] * e_neg
        e_last = e_pos[lc - 1:lc, :]
        full = (kk_ref[bi] * jnp.exp(c - lw), r_ref[bi] * e_pos, k_i, b_i, v_ref[bi],
                k_i * e_last, b_i * e_last)
        for h in range(RWKV_HEADS):
            hs = slice(h * RWKV_HS, (h + 1) * RWKV_HS)
            for dst, z in zip((kk_h, r_h, k_h, b_h, v_h, ke_h, be_h), full):
                dst.append(z[:, hs].astype(BF16))
            el_h.append(e_last[:, hs])
    idx = [(bi, h) for bi in range(bb) for h in range(RWKV_HEADS)]
    a_kb = [jnp.where(strict, _mm(x, y, _NT), 0.0) for x, y in zip(kk_h, b_h)]
    a_kk = [jnp.where(strict, _mm(x, y, _NT), 0.0) for x, y in zip(kk_h, k_h)]
    a_rb = [jnp.where(incl, _mm(x, y, _NT), 0.0) for x, y in zip(r_h, b_h)]
    a_rk = [jnp.where(incl, _mm(x, y, _NT), 0.0) for x, y in zip(r_h, k_h)]
    s0 = [s_ref[bi, h] for bi, h in idx]
    s0b = [x.astype(BF16) for x in s0]
    rhs = [_mm(x, s, _NT) + _mm(a, v) for x, s, a, v in zip(kk_h, s0b, a_kk, v_h)]
    o_part = [_mm(x, s, _NT) + _mm(a, v) for x, s, a, v in zip(r_h, s0b, a_rk, v_h)]
    inv = _unit_lower_inverse(a_kb, ti, si, eye)
    u = [_mm(x, y).astype(BF16) for x, y in zip(inv, rhs)]
    outs = [x - _mm(a, y) for x, a, y in zip(o_part, a_rb, u)]
    for n, (bi, h) in enumerate(idx):
        s_ref[bi, h] = s0[n] * el_h[n] + _mm(v_h[n], ke_h[n], _TN) - _mm(u[n], be_h[n], _TN)
    for bi in range(bb):
        o_ref[bi] = jnp.concatenate(outs[bi * RWKV_HEADS:(bi + 1) * RWKV_HEADS], axis=-1)


def wkv_scan(r, lw, k, v, kk, b, s0, bb, lc):
    bsz, t, _ = r.shape
    assert bsz % bb == 0 and t % lc == 0
    tok = pl.BlockSpec((bb, lc, W_RWKV), lambda i, j: (i, j, 0))
    st = pl.BlockSpec((bb, RWKV_HEADS, RWKV_HS, RWKV_HS), lambda i, j: (i, 0, 0, 0))
    return pl.pallas_call(
        functools.partial(_wkv_kernel, bb=bb, lc=lc),
        grid=(bsz // bb, t // lc),
        in_specs=[tok] * 6 + [st],
        out_specs=[tok, st],
        out_shape=[jax.ShapeDtypeStruct((bsz, t, W_RWKV), F32),
                   jax.ShapeDtypeStruct(s0.shape, F32)],
        compiler_params=_params("parallel", "arbitrary"),
        name="wkv_scan",
    )(r, lw, k, v, kk, b, s0)


def _mix_out_kernel(*refs, mode, bb, out_scale):
    if mode == "a":
        (o_ref, bv_ref, tail_ref, h_ref, mk_ref, mv_ref, wout_ref, mqg_ref, g_ref, bias_ref,
         out_ref) = refs
    else:
        o_ref, tail_ref, h_ref, mk_ref, mv_ref, wout_ref, mqg_ref, g_ref, out_ref = refs
    tq = o_ref.shape[1]
    rows = bb * tq
    o = o_ref[...].reshape(rows, W_RWKV)
    if mode == "a":
        d = o - _seg_sum64(o) * (1.0 / RWKV_HS)
        var = _seg_sum64(d * d) * (1.0 / RWKV_HS)
        o = d * lax.rsqrt(var + LNX_EPS) * g_ref[...] + bias_ref[...] + bv_ref[...].reshape(rows, W_RWKV)
    else:
        parts = []
        for hh in range(DIFF_HEADS):
            ob = o[:, LANES * hh:LANES * (hh + 1)]
            parts.append(ob * lax.rsqrt(jnp.mean(ob * ob, axis=-1, keepdims=True) + NORM_EPS))
        o = jnp.concatenate(parts, axis=-1) * g_ref[...] * out_scale

    tail = tail_ref[...].reshape(rows, TAIL)
    mq = tail[:, :W_MEM]
    gate = tail[:, W_MEM:]
    qn = mq * lax.rsqrt(_seg_sum64(mq * mq) * (1.0 / MEM_HD) + NORM_EPS) * mqg_ref[...]
    lane = lax.broadcasted_iota(jnp.int32, (tq, W_MEM), 1)
    mos = []
    for bi in range(bb):
        kb = mk_ref[bi].astype(BF16)
        vb = mv_ref[bi].astype(BF16)
        qb = qn[bi * tq:(bi + 1) * tq]
        mo = jnp.zeros((tq, W_MEM), F32)
        for n in range(MEM_HEADS):
            in_head = (lane >= n * MEM_HD) & (lane < (n + 1) * MEM_HD)
            s = lax.dot_general(jnp.where(in_head, qb, 0.0).astype(BF16), kb, _NT,
                                preferred_element_type=F32) * MEM_HD ** -0.5
            p = jnp.exp(s - jnp.max(s, axis=-1, keepdims=True))
            p = p / jnp.sum(p, axis=-1, keepdims=True)
            mo = jnp.where(in_head, jnp.dot(p.astype(BF16), vb, preferred_element_type=F32), mo)
        mos.append(mo)
    mo = mos[0] if bb == 1 else jnp.concatenate(mos, axis=0)

    silu = gate * jax.nn.sigmoid(gate)
    y_o = (o * silu[:, :W_RWKV]).astype(BF16)
    y_m = (mo * silu[:, W_RWKV:]).astype(BF16)
    res = (h_ref[...].reshape(rows, D_MODEL)
           + jnp.dot(y_o, wout_ref[0:W_RWKV, :], preferred_element_type=F32)
           + jnp.dot(y_m, wout_ref[W_RWKV:MIX, :], preferred_element_type=F32))
    out_ref[...] = res.reshape(bb, tq, D_MODEL)


def mix_out(mode, o, bv, tail, h, mem_k, mem_v, w_out, mq_g, g, bias, out_scale, bb, tq):
    bsz, t, _ = o.shape
    assert bsz % bb == 0 and t % tq == 0
    tok = lambda c: pl.BlockSpec((bb, tq, c), lambda i, j: (i, j, 0))
    const = lambda shape: pl.BlockSpec(shape, lambda i, j: (0, 0))
    mem = pl.BlockSpec((bb, mem_k.shape[1], W_MEM), lambda i, j: (i, 0, 0))
    row = lambda z: z.reshape(1, -1)
    in_specs = [tok(W_RWKV)] + ([tok(W_RWKV)] if mode == "a" else []) + [
        tok(TAIL), tok(D_MODEL), mem, mem, const((MIX, D_MODEL)), const((1, W_MEM)), const((1, W_RWKV))]
    args = [o] + ([bv] if mode == "a" else []) + [
        tail, h, mem_k, mem_v, w_out, row(jnp.tile(mq_g, MEM_HEADS)), row(g)]
    if mode == "a":
        in_specs.append(const((1, W_RWKV)))
        args.append(row(bias))
    return pl.pallas_call(
        functools.partial(_mix_out_kernel, mode=mode, bb=bb, out_scale=out_scale),
        grid=(bsz // bb, t // tq),
        in_specs=in_specs,
        out_specs=tok(D_MODEL),
        out_shape=jax.ShapeDtypeStruct((bsz, t, D_MODEL), F32),
        compiler_params=_params("parallel", "parallel"),
        name="mix_out_" + mode,
    )(*args)


def _qk_rope_kernel(*refs, rope):
    if rope:
        x_ref, g_ref, cos_ref, sin_ref, o_ref = refs
    else:
        x_ref, g_ref, o_ref = refs
    x = x_ref[...]
    y = x * lax.rsqrt(_seg_sum64(x * x) * (1.0 / 64) + NORM_EPS) * g_ref[...]
    if rope:
        tm, c = y.shape
        first_half = lax.broadcasted_iota(jnp.int32, (tm, LANES), 1) % 64 < 32
        parts = []
        for p in range(c // LANES):
            yb = y[:, LANES * p:LANES * (p + 1)]
            parts.append(jnp.where(first_half, pltpu.roll(yb, LANES - 32, 1), pltpu.roll(yb, 32, 1)))
        y = y * cos_ref[...] + jnp.concatenate(parts, axis=-1) * sin_ref[...]
    o_ref[...] = y


def qk_norm_rope(x, g, cos=None, sin=None, tm=256):
    bsz, t, c = x.shape
    tm = min(tm, t)
    assert t % tm == 0
    rope = cos is not None
    tok = pl.BlockSpec((None, tm, c), lambda b, i: (b, i, 0))
    in_specs = [tok, pl.BlockSpec((1, c), lambda b, i: (0, 0))]
    args = [x, jnp.tile(g, c // g.shape[0]).reshape(1, c)]
    if rope:
        in_specs += [pl.BlockSpec((tm, c), lambda b, i: (i, 0))] * 2
        args += [cos, sin]
    return pl.pallas_call(
        functools.partial(_qk_rope_kernel, rope=rope),
        grid=(bsz, t // tm),
        in_specs=in_specs,
        out_specs=tok,
        out_shape=jax.ShapeDtypeStruct(x.shape, F32),
        compiler_params=_params("parallel", "parallel"),
        name="qk_norm_rope",
    )(*args)


def rope_tables(pos, width):
    half = DIFF_HD // 2
    inv = ROPE_THETA ** (-jnp.arange(half, dtype=F32) / half)
    ang = pos.astype(F32)[:, None] * inv[None, :]
    cos, sin = jnp.cos(ang), jnp.sin(ang)
    reps = width // DIFF_HD
    return (jnp.tile(jnp.concatenate([cos, cos], axis=-1), (1, reps)),
            jnp.tile(jnp.concatenate([-sin, sin], axis=-1), (1, reps)))


def _online_update(s, m, l, acc, vb):
    m_new = jnp.maximum(m, jnp.max(s, axis=-1, keepdims=True))
    alpha = jnp.exp(m - m_new)
    p = jnp.exp(s - m_new)
    l = alpha * l + jnp.sum(p, axis=-1, keepdims=True)
    acc = alpha * acc + jnp.dot(p.astype(BF16), vb, preferred_element_type=F32)
    return m_new, l, acc


def _diff_attn_prompt_kernel(lam_ref, q_ref, k_ref, v_ref, o_ref, *, tq):
    i = pl.program_id(2)
    q = q_ref[...]
    left = lax.broadcasted_iota(jnp.int32, (tq, LANES), 1) < DIFF_HD
    qs = (jnp.where(left, q, 0.0).astype(BF16), jnp.where(left, 0.0, q).astype(BF16))
    qpos = i * tq + lax.broadcasted_iota(jnp.int32, (tq, tq), 0)
    kidx = lax.broadcasted_iota(jnp.int32, (tq, tq), 1)

    def body(j, carry):
        off = pl.multiple_of(j * tq, tq)
        kb = k_ref[pl.ds(off, tq), :].astype(BF16)
        vb = v_ref[pl.ds(off, tq), :].astype(BF16)
        causal = qpos >= j * tq + kidx
        out = []
        for c in range(2):
            s = lax.dot_general(qs[c], kb, _NT, preferred_element_type=F32) * DIFF_HD ** -0.5
            out.extend(_online_update(jnp.where(causal, s, NEG_INF), *carry[3 * c:3 * c + 3], vb))
        return tuple(out)

    init = (jnp.full((tq, 1), NEG_INF, F32), jnp.zeros((tq, 1), F32), jnp.zeros((tq, LANES), F32)) * 2
    _, l0, a0, _, l1, a1 = lax.fori_loop(0, i + 1, body, init)
    o_ref[...] = a0 / l0 - lam_ref[0] * (a1 / l1)


def diff_attn_prompt(q, k, v, lam, tq=256):
    bsz, t, _ = q.shape
    tq = min(tq, t)
    assert t % tq == 0
    qo = pl.BlockSpec((None, tq, LANES), lambda b, h, i: (b, i, h))
    kv = pl.BlockSpec((None, t, LANES), lambda b, h, i: (b, 0, h))
    return pl.pallas_call(
        functools.partial(_diff_attn_prompt_kernel, tq=tq),
        grid=(bsz, DIFF_HEADS, t // tq),
        in_specs=[pl.BlockSpec(memory_space=pltpu.SMEM), qo, kv, kv],
        out_specs=qo,
        out_shape=jax.ShapeDtypeStruct(q.shape, F32),
        compiler_params=_params("parallel", "parallel", "parallel"),
        name="diff_attn_prompt",
    )(lam.reshape(1), q, k, v)


def _diff_attn_sample_kernel(pt_ref, lam_ref, q_ref, kn_ref, vn_ref, ck_ref, cv_ref, o_ref,
                             qs_ref, m_ref, l_ref, acc_ref, *, n_pages):
    p = pl.program_id(1)
    t = q_ref.shape[0]
    n_rows = 2 * DIFF_HEADS * t

    @pl.when(p == 0)
    def _():
        q = q_ref[...]
        lane = lax.broadcasted_iota(jnp.int32, q.shape, 1)
        blocks = []
        for c in range(2):
            for h in range(DIFF_HEADS):
                lo = h * LANES + c * DIFF_HD
                blocks.append(jnp.where((lane >= lo) & (lane < lo + DIFF_HD), q, 0.0))
        qs_ref[...] = jnp.concatenate(blocks, axis=0).astype(BF16)
        m_ref[...] = jnp.full(m_ref.shape, NEG_INF, F32)
        l_ref[...] = jnp.zeros(l_ref.shape, F32)
        acc_ref[...] = jnp.zeros(acc_ref.shape, F32)

    def accumulate(kb, vb, mask):
        s = lax.dot_general(qs_ref[...], kb, _NT, preferred_element_type=F32) * DIFF_HD ** -0.5
        if mask is not None:
            s = jnp.where(mask, s, NEG_INF)
        m_ref[...], l_ref[...], acc_ref[...] = _online_update(s, m_ref[...], l_ref[...], acc_ref[...], vb)

    accumulate(ck_ref[...].astype(BF16), cv_ref[...].astype(BF16), None)

    @pl.when(p == n_pages - 1)
    def _():
        page = ck_ref.shape[0]
        pad = jnp.zeros((page - t, W_DIFF), F32)
        kn = jnp.concatenate([kn_ref[...], pad], axis=0).astype(BF16)
        vn = jnp.concatenate([vn_ref[...], pad], axis=0).astype(BF16)
        qt = lax.broadcasted_iota(jnp.int32, (n_rows, page), 0) % t
        kt = lax.broadcasted_iota(jnp.int32, (n_rows, page), 1)
        accumulate(kn, vn, qt >= kt)
        res = acc_ref[...] / l_ref[...]
        for h in range(DIFF_HEADS):
            r0, r1 = h * t, (DIFF_HEADS + h) * t
            cols = slice(h * LANES, (h + 1) * LANES)
            o_ref[:, cols] = res[r0:r0 + t, cols] - lam_ref[0] * res[r1:r1 + t, cols]


def diff_attn_sample(q, k_new, v_new, cache_k, cache_v, page_table, lam):
    bsz, t, _ = q.shape
    n_pages = page_table.shape[1]
    page = cache_k.shape[1]
    n_rows = 2 * DIFF_HEADS * t
    tok = pl.BlockSpec((None, t, W_DIFF), lambda b, p, pt: (b, 0, 0))
    pg = pl.BlockSpec((None, page, W_DIFF), lambda b, p, pt: (pt[b * n_pages + p], 0, 0))
    return pl.pallas_call(
        functools.partial(_diff_attn_sample_kernel, n_pages=n_pages),
        grid_spec=pltpu.PrefetchScalarGridSpec(
            num_scalar_prefetch=1,
            grid=(bsz, n_pages),
            in_specs=[pl.BlockSpec(memory_space=pltpu.SMEM), tok, tok, tok, pg, pg],
            out_specs=tok,
            scratch_shapes=[pltpu.VMEM((n_rows, W_DIFF), BF16), pltpu.VMEM((n_rows, 1), F32),
                            pltpu.VMEM((n_rows, 1), F32), pltpu.VMEM((n_rows, W_DIFF), F32)]),
        out_shape=jax.ShapeDtypeStruct(q.shape, F32),
        compiler_params=_params("parallel", "arbitrary"),
        name="diff_attn_sample",
    )(page_table.reshape(-1), lam.reshape(1), q, k_new, v_new,
      cache_k.reshape(cache_k.shape[0], page, W_DIFF), cache_v.reshape(cache_v.shape[0], page, W_DIFF))


def _pad_rows(w, rows_before, total=LANES):
    return jnp.pad(w, ((rows_before, total - rows_before - w.shape[0]), (0, 0))).astype(BF16)


def _trunk(h, pos, mem_layers, shift0, wkv0, attend_fn, p, scan_tiles, mix_tiles):
    bsz, t, _ = h.shape
    n_a = DEPTH // 2
    shifts, states = [], []
    v_first = None
    k_new = v_new = None
    cos, sin = rope_tables(pos, W_DIFF)
    for l in range(DEPTH):
        mk, mv = mem_layers[l]
        mk = mk.reshape(bsz, -1, W_MEM)
        mv = mv.reshape(bsz, -1, W_MEM)
        h2 = h.reshape(bsz * t, D_MODEL)
        w_out = p["w_out"][l].astype(BF16)
        if l < n_a:
            if l == 0:
                w_in = p["a_w_in_first"].astype(BF16)
                mu, csh = p["a_mu_first"], SHIFT_FIRST
                v0 = vup = None
            else:
                w = p["a_w_in_rest"][l - 1]
                w_in = jnp.concatenate(
                    [w[:, :SHIFT_REST], jnp.zeros((D_MODEL, SHIFT_REST_PAD - SHIFT_REST), F32),
                     w[:, SHIFT_REST:]], axis=1).astype(BF16)
                mu = jnp.pad(p["a_mu_rest"][l - 1], (0, SHIFT_REST_PAD - SHIFT_REST))
                csh = SHIFT_REST_PAD
                v0, vup = p["a_v0"][l - 1], _pad_rows(p["a_v_up"][l - 1], 0)
            psh, tail = norm_matmul(h2, p["norm_g"][l], w_in, (csh, TAIL))
            if shift0 is None:
                p0 = jnp.zeros((bsz, 1, csh), F32)
            else:
                (p0,) = norm_matmul(shift0[l], p["norm_g"][l], w_in[:, :csh], (csh,), do_norm=False)
                p0 = p0.reshape(bsz, 1, csh)
            r, w, k, v, kk, b, bv = a_prep(
                psh.reshape(bsz, t, csh), p0, mu, p["a_w0"][l], _pad_rows(p["a_w_up"][l], 0),
                p["a_a0"][l], _pad_rows(p["a_a_up"][l], LORA_W), p["a_k_k"][l], p["a_k_a"][l],
                p["a_r_k"][l], v0, vup, v_first)
            if l == 0:
                v_first = v
            s0 = jnp.zeros((bsz, RWKV_HEADS, RWKV_HS, RWKV_HS), F32) if wkv0 is None else wkv0[l]
            o, s_new = wkv_scan(r, w, k, v, kk, b, s0, *scan_tiles)
            states.append(s_new)
            shifts.append(rms_rows(h[:, -1, :], p["norm_g"][l]))
            h = mix_out("a", o, bv, tail.reshape(bsz, t, TAIL), h, mk, mv, w_out, p["mem_q_g"][l],
                        p["a_lnx_g"][l], p["a_lnx_b"][l], 1.0, *mix_tiles)
            if l == n_a - 1:
                k_raw, v_new = norm_matmul(h.reshape(bsz * t, D_MODEL), p["kv_norm_g"],
                                           p["kv_w"].astype(BF16), (W_DIFF, W_DIFF))
                k_new = qk_norm_rope(k_raw.reshape(bsz, t, W_DIFF), p["kv_k_g"], cos, sin)
                v_new = v_new.reshape(bsz, t, W_DIFF)
        else:
            j = l - n_a
            lam_init = 0.8 - 0.6 * math.exp(-0.3 * l)
            lv = p["b_lam"][j].astype(F32)
            lam = jnp.exp(jnp.sum(lv[0] * lv[1])) - jnp.exp(jnp.sum(lv[2] * lv[3])) + lam_init
            q_raw, tail = norm_matmul(h2, p["norm_g"][l], p["b_w_in"][j].astype(BF16), (W_DIFF, TAIL))
            q = qk_norm_rope(q_raw.reshape(bsz, t, W_DIFF), p["b_q_g"][j], cos, sin)
            o = attend_fn(q, k_new, v_new, lam)
            h = mix_out("b", o, None, tail.reshape(bsz, t, TAIL), h, mk, mv, w_out, p["mem_q_g"][l],
                        jnp.tile(p["b_subln_g"][j], DIFF_HEADS), None, 1.0 - lam_init, *mix_tiles)
    return h, jnp.stack(shifts), jnp.stack(states), k_new, v_new


def kernel(x_prompt, x_sample, mem_prompt, state_wkv, state_shift, cache_k, cache_v, cache_mem_k,
           cache_mem_v, page_table, norm_g, a_w_in_first, a_mu_first, a_w_in_rest, a_mu_rest, a_w0,
           a_w_up, a_a0, a_a_up, a_v0, a_v_up, a_k_k, a_k_a, a_r_k, a_lnx_g, a_lnx_b, kv_norm_g,
           kv_w, kv_k_g, b_w_in, b_q_g, b_lam, b_subln_g, mem_norm_g, mem_w_kv, mem_k_g, mem_q_g,
           w_out):
    p = dict(norm_g=norm_g, a_w_in_first=a_w_in_first, a_mu_first=a_mu_first,
             a_w_in_rest=a_w_in_rest, a_mu_rest=a_mu_rest, a_w0=a_w0, a_w_up=a_w_up, a_a0=a_a0,
             a_a_up=a_a_up, a_v0=a_v0, a_v_up=a_v_up, a_k_k=a_k_k, a_k_a=a_k_a, a_r_k=a_r_k,
             a_lnx_g=a_lnx_g, a_lnx_b=a_lnx_b, kv_norm_g=kv_norm_g, kv_w=kv_w, kv_k_g=kv_k_g,
             b_w_in=b_w_in, b_q_g=b_q_g, b_lam=b_lam, b_subln_g=b_subln_g, mem_q_g=mem_q_g,
             w_out=w_out)
    bsz, seq, _ = x_prompt.shape
    dbsz, dec_seq, _ = x_sample.shape
    m_tok = mem_prompt.shape[1]
    past_len = page_table.shape[1] * cache_k.shape[1]

    mem_p = []
    for l in range(DEPTH):
        k_raw, v_m = norm_matmul(mem_prompt.reshape(bsz * m_tok, D_MODEL), mem_norm_g[l],
                                 mem_w_kv[l].astype(BF16), (W_MEM, W_MEM))
        k_m = qk_norm_rope(k_raw.reshape(bsz, m_tok, W_MEM), mem_k_g[l])
        mem_p.append((k_m.reshape(bsz, m_tok, MEM_HEADS, MEM_HD), v_m.reshape(bsz, m_tok, MEM_HEADS, MEM_HD)))
    tq = min(256, seq)
    y_p, shift_p, wkv_p, k_p, v_p = _trunk(
        x_prompt, jnp.arange(seq), mem_p, None, None, diff_attn_prompt, p,
        scan_tiles=(1, min(WKV_CHUNK, seq)), mix_tiles=(1, tq))

    mem_s = [(cache_mem_k[l], cache_mem_v[l]) for l in range(DEPTH)]
    attend_s = lambda q, k, v, lam: diff_attn_sample(q, k, v, cache_k, cache_v, page_table, lam)
    bb = math.gcd(dbsz, 16)
    y_s, shift_s, wkv_s, k_s, v_s = _trunk(
        x_sample, past_len + jnp.arange(dec_seq), mem_s, state_shift, state_wkv, attend_s, p,
        scan_tiles=(math.gcd(dbsz, 4), dec_seq), mix_tiles=(bb, dec_seq))

    return (y_p, y_s, wkv_p, shift_p,
            k_p.reshape(bsz, seq, DIFF_HEADS, 2, DIFF_HD), v_p.reshape(bsz, seq, DIFF_HEADS, 2 * DIFF_HD),
            jnp.stack([m[0] for m in mem_p]), jnp.stack([m[1] for m in mem_p]),
            wkv_s, shift_s,
            k_s.reshape(dbsz, dec_seq, DIFF_HEADS, 2, DIFF_HD),
            v_s.reshape(dbsz, dec_seq, DIFF_HEADS, 2 * DIFF_HD))
```
